```python
import jax, jax.numpy as jnp
from jax import lax
import numpy as np

D_MODEL = 1024
BATCH = 8
SEQ = 2048
DEPTH = 1

MIX_WIDTH = D_MODEL
HEAD_DIM = 64
ATTN_WIDTH = MIX_WIDTH // 2
N_Q_HEADS = ATTN_WIDTH // HEAD_DIM
N_KV_HEADS = 2
KV_GROUP = N_Q_HEADS // N_KV_HEADS
KV_WIDTH = N_KV_HEADS * HEAD_DIM
WINDOW = 128
ATTN_BLOCK = 128
LRU_WIDTH = MIX_WIDTH - ATTN_WIDTH
LRU_BLOCKS = 8
LRU_BLOCK_W = LRU_WIDTH // LRU_BLOCKS
CONV_WIDTH = 4
LRU_C = 8.0
IN_SPLITS = [ATTN_WIDTH, ATTN_WIDTH + KV_WIDTH, ATTN_WIDTH + 2 * KV_WIDTH, ATTN_WIDTH + 2 * KV_WIDTH + LRU_WIDTH]
IN_COLS = ATTN_WIDTH + 2 * KV_WIDTH + 2 * LRU_WIDTH
N_EXPERTS = 32
TOP_K = 4
D_EXPERT = D_MODEL
SWIGLU_LIMIT = 7.0
SWIGLU_ALPHA = 1.702
MOE_BLOCK = 256
EPS = 1e-6

kernel_name = "hymba_griffin_swa_sink_moe_adaln"


def rms_norm(x, g):
    xf = x.astype(jnp.float32)
    y = xf * lax.rsqrt(jnp.mean(xf * xf, axis=-1, keepdims=True) + EPS)
    return (y * g.astype(jnp.float32)).astype(x.dtype)


def causal_depthwise_conv(x, w, b):
    y = lax.conv_general_dilated(
        x, w[:, None, :].astype(x.dtype), window_strides=(1,),
        padding=[(CONV_WIDTH - 1, 0)], dimension_numbers=("NWC", "WIO", "NWC"),
        feature_group_count=x.shape[-1])
    return y + b


def rg_lru(x, w_a, b_a, w_x, b_x, lam):
    B, S, C = x.shape
    xf = x.astype(jnp.float32)
    xh = xf.reshape(B, S, LRU_BLOCKS, LRU_BLOCK_W)
    r = jax.nn.sigmoid(jnp.einsum("bshi,hij->bshj", xh, w_a.astype(jnp.float32)) + b_a).reshape(B, S, C)
    i = jax.nn.sigmoid(jnp.einsum("bshi,hij->bshj", xh, w_x.astype(jnp.float32)) + b_x).reshape(B, S, C)
    log_a = LRU_C * r * jax.nn.log_sigmoid(lam.astype(jnp.float32))
    a = jnp.exp(log_a)
    mult = jnp.sqrt(jnp.maximum(-jnp.expm1(2.0 * log_a), 0.0))
    u = mult * (i * xf)

    def combine(left, right):
        a1, b1 = left
        a2, b2 = right
        return a1 * a2, a2 * b1 + b2

    _, h = lax.associative_scan(combine, (a, u), axis=1)
    return h.astype(x.dtype)


def sliding_window_sink_attention(q, k, v, sinks):
    B, S = q.shape[0], q.shape[1]
    nblk = S // ATTN_BLOCK
    qb = q.reshape(B, nblk, ATTN_BLOCK, N_KV_HEADS, KV_GROUP, HEAD_DIM)

    def band(t):
        tp = jnp.pad(t, ((0, 0), (ATTN_BLOCK, 0), (0, 0), (0, 0)))
        prev = tp[:, :S].reshape(B, nblk, ATTN_BLOCK, N_KV_HEADS, HEAD_DIM)
        cur = t.reshape(B, nblk, ATTN_BLOCK, N_KV_HEADS, HEAD_DIM)
        return jnp.concatenate([prev, cur], axis=2)

    kw, vw = band(k), band(v)
    scores = jnp.einsum("bnqkgd,bnskd->bnkgqs", qb, kw).astype(jnp.float32) * (HEAD_DIM ** -0.5)
    qi = jnp.arange(ATTN_BLOCK)[:, None]
    sj = jnp.arange(2 * ATTN_BLOCK)[None, :]
    rel = qi + ATTN_BLOCK - sj
    key_pos = jnp.arange(nblk)[:, None, None] * ATTN_BLOCK - ATTN_BLOCK + sj[None]
    mask = ((rel >= 0) & (rel < WINDOW))[None] & (key_pos >= 0)
    scores = jnp.where(mask[None, :, None, None], scores, jnp.finfo(jnp.float32).min)
    sink = jnp.broadcast_to(sinks.astype(jnp.float32).reshape(1, 1, N_KV_HEADS, KV_GROUP, 1, 1),
                            scores.shape[:-1] + (1,))
    probs = jax.nn.softmax(jnp.concatenate([scores, sink], axis=-1), axis=-1)[..., :-1]
    out = jnp.einsum("bnkgqs,bnskd->bnqkgd", probs.astype(v.dtype), vw)
    return out.reshape(B, S, N_Q_HEADS * HEAD_DIM)


def parallel_mixer(h, w_in, conv_w, conv_b, w_rg_a, b_rg_a, w_rg_x, b_rg_x, lru_lambda,
                   sinks, g_attn_out, g_lru_out, w_out):
    B, S, _ = h.shape
    p = h @ w_in
    q, k, v, xr, xg = jnp.split(p, IN_SPLITS, axis=-1)
    attn = sliding_window_sink_attention(
        q.reshape(B, S, N_Q_HEADS, HEAD_DIM), k.reshape(B, S, N_KV_HEADS, HEAD_DIM),
        v.reshape(B, S, N_KV_HEADS, HEAD_DIM), sinks)
    xc = causal_depthwise_conv(xr, conv_w, conv_b)
    lru = rg_lru(xc, w_rg_a, b_rg_a, w_rg_x, b_rg_x, lru_lambda) * jax.nn.gelu(xg)
    merged = jnp.concatenate([rms_norm(attn, g_attn_out), rms_norm(lru, g_lru_out)], axis=-1)
    return merged @ w_out


def moe_ffn(h, w_router, b_router, w_gu, b_gu, w_down, b_down):
    B, S, D = h.shape
    T = B * S
    A = T * TOP_K
    R = ((A + N_EXPERTS * (MOE_BLOCK - 1) + MOE_BLOCK - 1) // MOE_BLOCK) * MOE_BLOCK
    nb = R // MOE_BLOCK
    hf = h.reshape(T, D)
    logits = (hf @ w_router + b_router).astype(jnp.float32)
    top_vals, top_idx = lax.top_k(logits, TOP_K)
    gates = jax.nn.softmax(top_vals, axis=-1)
    flat_e = top_idx.reshape(A).astype(jnp.int32)
    flat_tok = jnp.repeat(jnp.arange(T, dtype=jnp.int32), TOP_K)
    flat_w = gates.reshape(A)
    order = jnp.argsort(flat_e)
    se = flat_e[order]
    sizes = jnp.bincount(flat_e, length=N_EXPERTS)
    padded = ((sizes + MOE_BLOCK - 1) // MOE_BLOCK) * MOE_BLOCK
    pad_end = jnp.cumsum(padded)
    pad_start = pad_end - padded
    start = jnp.cumsum(sizes) - sizes
    dest = pad_start[se] + (jnp.arange(A, dtype=jnp.int32) - start[se])
    row_tok = jnp.zeros((R,), jnp.int32).at[dest].set(flat_tok[order])
    row_w = jnp.zeros((R,), jnp.float32).at[dest].set(flat_w[order])
    blk_e = jnp.minimum(jnp.searchsorted(pad_end, jnp.arange(nb) * MOE_BLOCK, side="right"),
                        N_EXPERTS - 1).astype(jnp.int32)
    xs = hf[row_tok].reshape(nb, MOE_BLOCK, D)

    def expert_block(args):
        xb, e = args
        gu = xb @ w_gu[e] + b_gu[e]
        gate = jnp.minimum(gu[:, :D_EXPERT], SWIGLU_LIMIT)
        up = jnp.clip(gu[:, D_EXPERT:], -SWIGLU_LIMIT, SWIGLU_LIMIT)
        glu = gate * jax.nn.sigmoid(SWIGLU_ALPHA * gate)
        return ((up + 1.0) * glu) @ w_down[e] + b_down[e]

    ys = lax.map(expert_block, (xs, blk_e)).reshape(R, D)
    ys = ys.astype(jnp.float32) * row_w[:, None]
    out = jax.ops.segment_sum(ys, row_tok, num_segments=T)
    return out.reshape(B, S, D).astype(h.dtype)


def setup_inputs(seed: int = 0) -> dict:
    key = jax.random.key(seed)
    ks = jax.random.split(key, 26)
    L, D, E, F = DEPTH, D_MODEL, N_EXPERTS, D_EXPERT
    nrm = lambda k, shape, s: jax.random.normal(k, shape, jnp.float32) * s
    a0 = jax.random.uniform(ks[13], (L, LRU_WIDTH), jnp.float32, minval=0.9, maxval=0.999)
    root = a0 ** (1.0 / LRU_C)
    return {
        "x": nrm(ks[0], (BATCH, SEQ, D), 1.0),
        "c": nrm(ks[1], (BATCH, D), 1.0),
        "w_ada": nrm(ks[2], (L, D, 6 * D), 0.5 * D ** -0.5),
        "b_ada": nrm(ks[3], (L, 6 * D), 0.02),
        "g_mix": 1.0 + nrm(ks[4], (L, D), 0.05),
        "w_in": nrm(ks[5], (L, D, IN_COLS), D ** -0.5),
        "conv_w": nrm(ks[6], (L, CONV_WIDTH, LRU_WIDTH), CONV_WIDTH ** -0.5),
        "conv_b": nrm(ks[7], (L, LRU_WIDTH), 0.02),
        "w_rg_a": nrm(ks[8], (L, LRU_BLOCKS, LRU_BLOCK_W, LRU_BLOCK_W), LRU_BLOCK_W ** -0.5),
        "b_rg_a": nrm(ks[9], (L, LRU_BLOCKS, LRU_BLOCK_W), 0.02),
        "w_rg_x": nrm(ks[10], (L, LRU_BLOCKS, LRU_BLOCK_W, LRU_BLOCK_W), LRU_BLOCK_W ** -0.5),
        "b_rg_x": nrm(ks[11], (L, LRU_BLOCKS, LRU_BLOCK_W), 0.02),
        "lru_lambda": jnp.log(root) - jnp.log1p(-root),
        "sinks": nrm(ks[12], (L, N_Q_HEADS), 0.5),
        "g_attn_out": 1.0 + nrm(ks[14], (L, ATTN_WIDTH), 0.05),
        "g_lru_out": 1.0 + nrm(ks[15], (L, LRU_WIDTH), 0.05),
        "w_out": nrm(ks[16], (L, MIX_WIDTH, D), MIX_WIDTH ** -0.5),
        "g_ffn": 1.0 + nrm(ks[17], (L, D), 0.05),
        "w_router": nrm(ks[18], (L, D, E), D ** -0.5),
        "b_router": nrm(ks[19], (L, E), 0.01),
        "w_gu": nrm(ks[20], (L, E, D, 2 * F), D ** -0.5),
        "b_gu": nrm(ks[21], (L, E, 2 * F), 0.02),
        "w_down": nrm(ks[22], (L, E, F, D), F ** -0.5),
        "b_down": nrm(ks[23], (L, E, D), 0.02),
        "g_final": 1.0 + nrm(ks[24], (D,), 0.05),
    }


def reference(x, c, w_ada, b_ada, g_mix, w_in, conv_w, conv_b, w_rg_a, b_rg_a, w_rg_x, b_rg_x,
              lru_lambda, sinks, g_attn_out, g_lru_out, w_out, g_ffn, w_router, b_router,
              w_gu, b_gu, w_down, b_down, g_final):
    for l in range(DEPTH):
        mod = (c @ w_ada[l] + b_ada[l])[:, None, :]
        sh1, sc1, gt1, sh2, sc2, gt2 = jnp.split(mod, 6, axis=-1)
        h = rms_norm(x, g_mix[l]) * (1.0 + sc1) + sh1
        x = x + gt1 * parallel_mixer(h, w_in[l], conv_w[l], conv_b[l], w_rg_a[l], b_rg_a[l],
                                     w_rg_x[l], b_rg_x[l], lru_lambda[l], sinks[l],
                                     g_attn_out[l], g_lru_out[l], w_out[l])
        h = rms_norm(x, g_ffn[l]) * (1.0 + sc2) + sh2
        x = x + gt2 * moe_ffn(h, w_router[l], b_router[l], w_gu[l], b_gu[l], w_down[l], b_down[l])
    return rms_norm(x, g_final)
```

```python
import functools

import jax
import jax.numpy as jnp
from jax import lax
from jax.experimental import pallas as pl
from jax.experimental.pallas import tpu as pltpu

D_MODEL = 1024
HEAD_DIM = 64
ATTN_WIDTH = 512
N_Q_HEADS = 8
N_KV_HEADS = 2
KV_GROUP = 4
KV_WIDTH = 128
WINDOW = 128
LRU_WIDTH = 512
LRU_BLOCKS = 8
LRU_BLOCK_W = 64
CONV_WIDTH = 4
LRU_C = 8.0
IN_COLS = ATTN_WIDTH + 2 * KV_WIDTH + 2 * LRU_WIDTH
N_EXPERTS = 32
TOP_K = 4
D_EXPERT = 1024
SWIGLU_LIMIT = 7.0
SWIGLU_ALPHA = 1.702
EPS = 1e-6

LANES = 128
MOE_ROWS = 256
VMEM_LIMIT = 56 * 1024 * 1024

F32 = jnp.float32
BF16 = jnp.bfloat16


def _rms(x):
    return x * lax.rsqrt(jnp.mean(x * x, axis=-1, keepdims=True) + EPS)


def _adaln_kernel(c_ref, w_ref, b_ref, o_ref):
    o_ref[...] = jnp.dot(c_ref[...].astype(BF16), w_ref[...].astype(BF16),
                         preferred_element_type=F32) + b_ref[...]


def _adaln(c, w, b):
    B, D = c.shape
    N = w.shape[1]
    tn = 1536
    return pl.pallas_call(
        _adaln_kernel,
        out_shape=jax.ShapeDtypeStruct((B, N), F32),
        grid=(N // tn,),
        in_specs=[pl.BlockSpec((B, D), lambda j: (0, 0)),
                  pl.BlockSpec((D, tn), lambda j: (0, j)),
                  pl.BlockSpec((1, tn), lambda j: (0, j))],
        out_specs=pl.BlockSpec((B, tn), lambda j: (0, j)),
        compiler_params=pltpu.CompilerParams(vmem_limit_bytes=VMEM_LIMIT),
        name="adaln",
    )(c, w, b.reshape(1, N))


def _in_proj_kernel(x_ref, mod_ref, g_ref, w_ref, q_ref, k_ref, v_ref, xr_ref, xg_ref):
    mod = mod_ref[0]
    sh1 = mod[:, 0:D_MODEL]
    sc1 = mod[:, D_MODEL:2 * D_MODEL]
    h = _rms(x_ref[...]) * g_ref[...] * (1.0 + sc1) + sh1
    p = jnp.dot(h.astype(BF16), w_ref[...], preferred_element_type=F32)
    q_ref[...] = (p[:, 0:ATTN_WIDTH] * (HEAD_DIM ** -0.5)).astype(BF16)
    k_ref[...] = p[:, ATTN_WIDTH:ATTN_WIDTH + KV_WIDTH].astype(BF16)
    v_ref[...] = p[:, ATTN_WIDTH + KV_WIDTH:ATTN_WIDTH + 2 * KV_WIDTH].astype(BF16)
    xr_ref[...] = p[:, ATTN_WIDTH + 2 * KV_WIDTH:ATTN_WIDTH + 2 * KV_WIDTH + LRU_WIDTH]
    xg_ref[...] = p[:, ATTN_WIDTH + 2 * KV_WIDTH + LRU_WIDTH:]


def _in_proj(x2, mod3, g, w_bf, seq):
    T, D = x2.shape
    tm = 512
    per_b = seq // tm
    row = lambda i: (i, 0)
    return pl.pallas_call(
        _in_proj_kernel,
        out_shape=(jax.ShapeDtypeStruct((T, ATTN_WIDTH), BF16),
                   jax.ShapeDtypeStruct((T, KV_WIDTH), BF16),
                   jax.ShapeDtypeStruct((T, KV_WIDTH), BF16),
                   jax.ShapeDtypeStruct((T, LRU_WIDTH), F32),
                   jax.ShapeDtypeStruct((T, LRU_WIDTH), F32)),
        grid=(T // tm,),
        in_specs=[pl.BlockSpec((tm, D), row),
                  pl.BlockSpec((1, 1, 6 * D), lambda i: (i // per_b, 0, 0)),
                  pl.BlockSpec((1, D), lambda i: (0, 0)),
                  pl.BlockSpec((D, IN_COLS), lambda i: (0, 0))],
        out_specs=(pl.BlockSpec((tm, ATTN_WIDTH), row),
                   pl.BlockSpec((tm, KV_WIDTH), row),
                   pl.BlockSpec((tm, KV_WIDTH), row),
                   pl.BlockSpec((tm, LRU_WIDTH), row),
                   pl.BlockSpec((tm, LRU_WIDTH), row)),
        compiler_params=pltpu.CompilerParams(vmem_limit_bytes=VMEM_LIMIT),
        name="in_proj",
    )(x2, mod3, g.reshape(1, D), w_bf)


def _attn_kernel(sink_ref, q_ref, kp_ref, kc_ref, vp_ref, vc_ref, o_ref):
    n = pl.program_id(1)
    q = q_ref[...]
    kw = jnp.concatenate([kp_ref[...], kc_ref[...]], axis=0)
    vw = jnp.concatenate([vp_ref[...], vc_ref[...]], axis=0)
    lo = lax.broadcasted_iota(jnp.int32, (WINDOW, LANES), 1) < HEAD_DIM
    zero = jnp.zeros((WINDOW, LANES), BF16)
    parts = []
    for s in range(KV_GROUP):
        slab = q[:, s * LANES:(s + 1) * LANES]
        parts.append(jnp.where(lo, slab, zero))
        parts.append(jnp.where(lo, zero, slab))
    qs = jnp.concatenate(parts, axis=0)
    scores = lax.dot_general(qs, kw, (((1,), (1,)), ((), ())), preferred_element_type=F32)
    qi = lax.broadcasted_iota(jnp.int32, (WINDOW, 2 * WINDOW), 0)
    cj = lax.broadcasted_iota(jnp.int32, (WINDOW, 2 * WINDOW), 1)
    rel = qi + WINDOW - cj
    valid = (rel >= 0) & (rel < WINDOW) & ((cj >= WINDOW) | (n > 0))
    neg = jnp.finfo(F32).min
    outs = []
    for hb in range(N_Q_HEADS):
        head = (hb // 2) + KV_GROUP * (hb % 2)
        s_h = jnp.where(valid, scores[hb * WINDOW:(hb + 1) * WINDOW], neg)
        sink = sink_ref[head]
        m = jnp.maximum(jnp.max(s_h, axis=1, keepdims=True), sink)
        p = jnp.exp(s_h - m)
        denom = jnp.sum(p, axis=1, keepdims=True) + jnp.exp(sink - m)
        o_h = jnp.dot(p.astype(BF16), vw, preferred_element_type=F32)
        outs.append(o_h * (1.0 / denom))
    for s in range(KV_GROUP):
        o_ref[:, s * LANES:(s + 1) * LANES] = jnp.where(lo, outs[2 * s], outs[2 * s + 1])


def _attention(q, k, v, sinks, batch, seq):
    T = q.shape[0]
    nblk = seq // WINDOW
    cur = lambda b, n, s: (b * nblk + n, 0)
    prev = lambda b, n, s: (b * nblk + jnp.maximum(n - 1, 0), 0)
    return pl.pallas_call(
        _attn_kernel,
        out_shape=jax.ShapeDtypeStruct((T, ATTN_WIDTH), F32),
        grid_spec=pltpu.PrefetchScalarGridSpec(
            num_scalar_prefetch=1,
            grid=(batch, nblk),
            in_specs=[pl.BlockSpec((WINDOW, ATTN_WIDTH), cur),
                      pl.BlockSpec((WINDOW, KV_WIDTH), prev),
                      pl.BlockSpec((WINDOW, KV_WIDTH), cur),
                      pl.BlockSpec((WINDOW, KV_WIDTH), prev),
                      pl.BlockSpec((WINDOW, KV_WIDTH), cur)],
            out_specs=pl.BlockSpec((WINDOW, ATTN_WIDTH), cur)),
        compiler_params=pltpu.CompilerParams(vmem_limit_bytes=VMEM_LIMIT),
        name="attention",
    )(sinks, q, k, k, v, v)


def _rglru_kernel(xr_ref, xg_ref, cw_ref, cb_ref, wa_ref, ba_ref, wx_ref, bx_ref, lam_ref,
                  o_ref, xprev_ref, hcar_ref):
    ts = xr_ref.shape[0]

    @pl.when(pl.program_id(1) == 0)
    def _():
        xprev_ref[...] = jnp.zeros_like(xprev_ref)
        hcar_ref[...] = jnp.zeros_like(hcar_ref)

    xr = xr_ref[...]
    xpad = jnp.concatenate([xprev_ref[...], xr], axis=0)
    xprev_ref[...] = xr[ts - 8:ts]
    cw = cw_ref[...]
    y = cw[CONV_WIDTH - 1:CONV_WIDTH] * xr + cb_ref[...]
    for d in range(1, CONV_WIDTH):
        y = y + cw[CONV_WIDTH - 1 - d:CONV_WIDTH - d] * pltpu.roll(xpad, d, 0)[8:]
    yb = y.astype(BF16)
    half = LRU_WIDTH // 2
    ra = jnp.concatenate([jnp.dot(yb[:, h * half:(h + 1) * half], wa_ref[h], preferred_element_type=F32)
                          for h in range(2)], axis=1) + ba_ref[...]
    rx = jnp.concatenate([jnp.dot(yb[:, h * half:(h + 1) * half], wx_ref[h], preferred_element_type=F32)
                          for h in range(2)], axis=1) + bx_ref[...]
    r = jax.nn.sigmoid(ra)
    gi = jax.nn.sigmoid(rx)
    lam = lam_ref[...]
    log_sig = -(jnp.maximum(-lam, 0.0) + jnp.log1p(jnp.exp(-jnp.abs(lam))))
    log_a = LRU_C * r * log_sig
    a = jnp.exp(log_a)
    mult = jnp.sqrt(jnp.maximum(-jnp.tanh(log_a) * (a * a + 1.0), 0.0))
    u = mult * (gi * y)
    row = lax.broadcasted_iota(jnp.int32, (ts, LRU_WIDTH), 0)
    A, U = a, u
    d = 1
    while d < ts:
        keep = row >= d
        U = A * jnp.where(keep, pltpu.roll(U, d, 0), 0.0) + U
        A = A * jnp.where(keep, pltpu.roll(A, d, 0), 1.0)
        d *= 2
    h = U + A * hcar_ref[0:1]
    hcar_ref[...] = jnp.broadcast_to(h[ts - 1:ts], hcar_ref.shape)
    o_ref[...] = h * jax.nn.gelu(xg_ref[...])


def _rglru(xr, xg, conv_w, conv_b, wa_bd, ba, wx_bd, bx, lam, batch, seq):
    T = xr.shape[0]
    ts = 256
    nblk = seq // ts
    row = lambda b, n: (b * nblk + n, 0)
    vec = lambda b, n: (0, 0)
    mat = lambda b, n: (0, 0, 0)
    half = LRU_WIDTH // 2
    return pl.pallas_call(
        _rglru_kernel,
        out_shape=jax.ShapeDtypeStruct((T, LRU_WIDTH), F32),
        grid=(batch, nblk),
        in_specs=[pl.BlockSpec((ts, LRU_WIDTH), row),
                  pl.BlockSpec((ts, LRU_WIDTH), row),
                  pl.BlockSpec((CONV_WIDTH, LRU_WIDTH), vec),
                  pl.BlockSpec((1, LRU_WIDTH), vec),
                  pl.BlockSpec((2, half, half), mat),
                  pl.BlockSpec((1, LRU_WIDTH), vec),
                  pl.BlockSpec((2, half, half), mat),
                  pl.BlockSpec((1, LRU_WIDTH), vec),
                  pl.BlockSpec((1, LRU_WIDTH), vec)],
        out_specs=pl.BlockSpec((ts, LRU_WIDTH), row),
        scratch_shapes=[pltpu.VMEM((8, LRU_WIDTH), F32), pltpu.VMEM((8, LRU_WIDTH), F32)],
        compiler_params=pltpu.CompilerParams(
            dimension_semantics=("arbitrary", "arbitrary"), vmem_limit_bytes=VMEM_LIMIT),
        name="rglru",
    )(xr, xg, conv_w, conv_b.reshape(1, -1), wa_bd, ba.reshape(1, -1), wx_bd, bx.reshape(1, -1),
      lam.reshape(1, -1))


def _split3(x):
    hi = x.astype(BF16)
    lo = (x - hi.astype(F32)).astype(BF16)
    return hi, lo


def _out_router_kernel(attn_ref, lru_ref, x_ref, mod_ref, ga_ref, gl_ref, wo_ref, gf_ref, wr_ref, br_ref,
                       x1_ref, h2_ref, topi_ref, gcol_ref):
    tm = x_ref.shape[0]
    mod = mod_ref[0]
    gt1 = mod[:, 2 * D_MODEL:3 * D_MODEL]
    sh2 = mod[:, 3 * D_MODEL:4 * D_MODEL]
    sc2 = mod[:, 4 * D_MODEL:5 * D_MODEL]
    a = (_rms(attn_ref[...]) * ga_ref[...]).astype(BF16)
    l = (_rms(lru_ref[...]) * gl_ref[...]).astype(BF16)
    mix = (jnp.dot(a, wo_ref[0:ATTN_WIDTH, :], preferred_element_type=F32)
           + jnp.dot(l, wo_ref[ATTN_WIDTH:, :], preferred_element_type=F32))
    x1 = x_ref[...] + gt1 * mix
    x1_ref[...] = x1
    h2 = _rms(x1) * gf_ref[...] * (1.0 + sc2) + sh2
    h2_ref[...] = h2
    nt = (((1,), (1,)), ((), ()))
    h_hi, h_lo = _split3(h2)
    w_hi, w_lo = _split3(wr_ref[...])
    logits = (lax.dot_general(w_hi, h_hi, nt, preferred_element_type=F32)
              + lax.dot_general(w_hi, h_lo, nt, preferred_element_type=F32)
              + lax.dot_general(w_lo, h_hi, nt, preferred_element_type=F32)) + br_ref[...]
    eidx = lax.broadcasted_iota(jnp.int32, (N_EXPERTS, tm), 0)
    vals, idxs = [], []
    for _ in range(TOP_K):
        m = jnp.max(logits, axis=0, keepdims=True)
        sel = jnp.min(jnp.where(logits == m, eidx, N_EXPERTS), axis=0, keepdims=True)
        vals.append(m)
        idxs.append(sel)
        logits = jnp.where(eidx == sel, -jnp.inf, logits)
    es = [jnp.exp(v - vals[0]) for v in vals]
    inv = 1.0 / (es[0] + es[1] + es[2] + es[3])
    topi_ref[...] = jnp.concatenate(idxs, axis=0)
    gates = jnp.concatenate([e * inv for e in es] + [jnp.zeros((LANES - TOP_K, tm), F32)], axis=0)
    gcol_ref[...] = gates.T


def _out_router(attn, lru, x2, mod3, ga, gl, wo_bf, gf, wr_t, br, seq):
    T, D = x2.shape
    tm = 256
    per_b = seq // tm
    row = lambda i: (i, 0)
    vec = lambda i: (0, 0)
    return pl.pallas_call(
        _out_router_kernel,
        out_shape=(jax.ShapeDtypeStruct((T, D), F32),
                   jax.ShapeDtypeStruct((T, D), F32),
                   jax.ShapeDtypeStruct((TOP_K, T), jnp.int32),
                   jax.ShapeDtypeStruct((T, LANES), F32)),
        grid=(T // tm,),
        in_specs=[pl.BlockSpec((tm, ATTN_WIDTH), row),
                  pl.BlockSpec((tm, LRU_WIDTH), row),
                  pl.BlockSpec((tm, D), row),
                  pl.BlockSpec((1, 1, 6 * D), lambda i: (i // per_b, 0, 0)),
                  pl.BlockSpec((1, ATTN_WIDTH), vec),
                  pl.BlockSpec((1, LRU_WIDTH), vec),
                  pl.BlockSpec((D, D), vec),
                  pl.BlockSpec((1, D), vec),
                  pl.BlockSpec((N_EXPERTS, D), vec),
                  pl.BlockSpec((N_EXPERTS, 1), vec)],
        out_specs=(pl.BlockSpec((tm, D), row),
                   pl.BlockSpec((tm, D), row),
                   pl.BlockSpec((TOP_K, tm), lambda i: (0, i)),
                   pl.BlockSpec((tm, LANES), row)),
        compiler_params=pltpu.CompilerParams(vmem_limit_bytes=VMEM_LIMIT),
        name="out_router",
    )(attn, lru, x2, mod3, ga.reshape(1, -1), gl.reshape(1, -1), wo_bf, gf.reshape(1, -1), wr_t,
      br.reshape(-1, 1))


def _moe_kernel(blk_e_ref, tok0_ref, tokn_ref, dst_ref, h_hbm, wgu_ref, bgu_ref, wd_ref, bd_ref,
                y_hbm, xbuf, ybuf, wgu_bf, wd_bf, act, gsem, ssem):
    i = pl.program_id(0)
    nb = pl.num_programs(0)
    slot = i % 2
    other = 1 - slot

    def gather(tok_ref, s):
        def body(j, c):
            pltpu.make_async_copy(h_hbm.at[pl.ds(tok_ref[0, 0, j], 1)],
                                  xbuf.at[s, pl.ds(j, 1)], gsem.at[s]).start()
            return c
        lax.fori_loop(0, MOE_ROWS, body, 0, unroll=8)

    def wait_gather(s):
        pltpu.make_async_copy(h_hbm.at[pl.ds(0, MOE_ROWS)], xbuf.at[s], gsem.at[s]).wait()

    def wait_scatter(s):
        pltpu.make_async_copy(ybuf.at[s], y_hbm.at[pl.ds(0, MOE_ROWS)], ssem.at[s]).wait()

    @pl.when(i == 0)
    def _():
        gather(tok0_ref, 0)

    gather(tokn_ref, other)
    wait_gather(slot)

    e_prev = blk_e_ref[jnp.maximum(i - 1, 0)]

    @pl.when((i == 0) | (blk_e_ref[i] != e_prev))
    def _():
        wgu_bf[...] = wgu_ref[0].astype(BF16)
        wd_bf[...] = wd_ref[0].astype(BF16)

    x = xbuf[slot].astype(BF16)
    cw = 512
    for c in range(D_EXPERT // cw):
        g = jnp.dot(x, wgu_bf[:, c * cw:(c + 1) * cw], preferred_element_type=F32) \
            + bgu_ref[0, :, c * cw:(c + 1) * cw]
        u = jnp.dot(x, wgu_bf[:, D_EXPERT + c * cw:D_EXPERT + (c + 1) * cw], preferred_element_type=F32) \
            + bgu_ref[0, :, D_EXPERT + c * cw:D_EXPERT + (c + 1) * cw]
        gate = jnp.minimum(g, SWIGLU_LIMIT)
        up = jnp.clip(u, -SWIGLU_LIMIT, SWIGLU_LIMIT)
        glu = gate * jax.nn.sigmoid(SWIGLU_ALPHA * gate)
        act[:, c * cw:(c + 1) * cw] = ((up + 1.0) * glu).astype(BF16)
    y = jnp.dot(act[...], wd_bf[...], preferred_element_type=F32) + bd_ref[0]

    @pl.when(i >= 2)
    def _():
        wait_scatter(slot)

    ybuf[slot] = y

    def sbody(j, c):
        pltpu.make_async_copy(ybuf.at[slot, pl.ds(j, 1)],
                              y_hbm.at[pl.ds(dst_ref[0, 0, j], 1)], ssem.at[slot]).start()
        return c
    lax.fori_loop(0, MOE_ROWS, sbody, 0, unroll=8)

    @pl.when(i == nb - 1)
    def _():
        wait_gather(other)
        wait_scatter(slot)

        @pl.when(nb >= 2)
        def _():
            wait_scatter(other)


def _moe(blk_e, row_tok, row_dst, h2, w_gu, b_gu, w_down, b_down, n_out_rows):
    NB = blk_e.shape[0]
    T, D = h2.shape
    E, _, F2 = w_gu.shape
    tok3 = row_tok.reshape(NB, 1, MOE_ROWS)
    dst3 = row_dst.reshape(NB, 1, MOE_ROWS)
    smem = functools.partial(pl.BlockSpec, memory_space=pltpu.SMEM)
    return pl.pallas_call(
        _moe_kernel,
        out_shape=jax.ShapeDtypeStruct((n_out_rows, D), F32),
        grid_spec=pltpu.PrefetchScalarGridSpec(
            num_scalar_prefetch=1,
            grid=(NB,),
            in_specs=[smem((1, 1, MOE_ROWS), lambda i, be: (0, 0, 0)),
                      smem((1, 1, MOE_ROWS), lambda i, be: (jnp.minimum(i + 1, NB - 1), 0, 0)),
                      smem((1, 1, MOE_ROWS), lambda i, be: (i, 0, 0)),
                      pl.BlockSpec(memory_space=pl.ANY),
                      pl.BlockSpec((1, D, F2), lambda i, be: (be[i], 0, 0)),
                      pl.BlockSpec((1, 1, F2), lambda i, be: (be[i], 0, 0)),
                      pl.BlockSpec((1, D_EXPERT, D), lambda i, be: (be[i], 0, 0)),
                      pl.BlockSpec((1, 1, D), lambda i, be: (be[i], 0, 0))],
            out_specs=pl.BlockSpec(memory_space=pl.ANY),
            scratch_shapes=[pltpu.VMEM((2, MOE_ROWS, D), F32),
                            pltpu.VMEM((2, MOE_ROWS, D), F32),
                            pltpu.VMEM((D, F2), BF16),
                            pltpu.VMEM((D_EXPERT, D), BF16),
                            pltpu.VMEM((MOE_ROWS, D_EXPERT), BF16),
                            pltpu.SemaphoreType.DMA((2,)),
                            pltpu.SemaphoreType.DMA((2,))]),
        compiler_params=pltpu.CompilerParams(
            dimension_semantics=("arbitrary",), vmem_limit_bytes=VMEM_LIMIT),
        name="moe",
    )(blk_e, tok3, tok3, dst3, h2, w_gu, b_gu.reshape(E, 1, F2), w_down, b_down.reshape(E, 1, D))


def _combine_kernel(x1_ref, y0_ref, y1_ref, y2_ref, y3_ref, g_ref, mod_ref, gfin_ref, o_ref):
    gt2 = mod_ref[0][:, 5 * D_MODEL:6 * D_MODEL]
    g = g_ref[...]
    moe = (g[:, 0:1] * y0_ref[...] + g[:, 1:2] * y1_ref[...]
           + g[:, 2:3] * y2_ref[...] + g[:, 3:4] * y3_ref[...])
    o_ref[...] = _rms(x1_ref[...] + gt2 * moe) * gfin_ref[...]


def _combine(x1, yt, gcol, mod3, gfin, seq, apply_final):
    T, D = x1.shape
    tm = 256
    per_b = seq // tm
    nt = T // tm
    row = lambda i: (i, 0)
    ys = [pl.BlockSpec((tm, D), functools.partial(lambda i, k: (k * nt + i, 0), k=k)) for k in range(TOP_K)]
    return pl.pallas_call(
        _combine_kernel if apply_final else _combine_nonorm_kernel,
        out_shape=jax.ShapeDtypeStruct((T, D), F32),
        grid=(nt,),
        in_specs=[pl.BlockSpec((tm, D), row)] + ys + [
            pl.BlockSpec((tm, LANES), row),
            pl.BlockSpec((1, 1, 6 * D), lambda i: (i // per_b, 0, 0)),
            pl.BlockSpec((1, D), lambda i: (0, 0))],
        out_specs=pl.BlockSpec((tm, D), row),
        compiler_params=pltpu.CompilerParams(vmem_limit_bytes=VMEM_LIMIT),
        name="combine",
    )(x1, yt, yt, yt, yt, gcol, mod3, gfin.reshape(1, D))


def _combine_nonorm_kernel(x1_ref, y0_ref, y1_ref, y2_ref, y3_ref, g_ref, mod_ref, gfin_ref, o_ref):
    gt2 = mod_ref[0][:, 5 * D_MODEL:6 * D_MODEL]
    g = g_ref[...]
    moe = (g[:, 0:1] * y0_ref[...] + g[:, 1:2] * y1_ref[...]
           + g[:, 2:3] * y2_ref[...] + g[:, 3:4] * y3_ref[...])
    o_ref[...] = x1_ref[...] + gt2 * moe


def _routing_tables(topi, T):
    A = T * TOP_K
    NB = A // MOE_ROWS + N_EXPERTS
    R = NB * MOE_ROWS
    e_flat = topi.reshape(A)
    onehot = (e_flat[:, None] == jnp.arange(N_EXPERTS, dtype=jnp.int32)[None, :]).astype(jnp.int32)
    csum = jnp.cumsum(onehot, axis=0)
    rank = jnp.sum(csum * onehot, axis=1) - 1
    sizes = csum[-1]
    padded = ((sizes + MOE_ROWS - 1) // MOE_ROWS) * MOE_ROWS
    pad_end = jnp.cumsum(padded)
    pad_start = pad_end - padded
    dest = pad_start[e_flat] + rank
    slot = jnp.arange(A, dtype=jnp.int32)
    tok = slot % T
    j = jnp.arange(R, dtype=jnp.int32)
    dump = A + ((j // MOE_ROWS) % 2) * MOE_ROWS + (j % MOE_ROWS)
    row_tok = jnp.zeros((R,), jnp.int32).at[dest].set(tok)
    row_dst = dump.at[dest].set(slot)
    blk_e = jnp.minimum(jnp.searchsorted(pad_end, jnp.arange(NB, dtype=jnp.int32) * MOE_ROWS, side="right"),
                        N_EXPERTS - 1).astype(jnp.int32)
    return blk_e, row_tok, row_dst


def _block_diag(w):
    half = LRU_BLOCKS // 2
    out = jnp.zeros((2, half * LRU_BLOCK_W, half * LRU_BLOCK_W), w.dtype)
    for b in range(LRU_BLOCKS):
        h, j = divmod(b, half)
        out = out.at[h, j * LRU_BLOCK_W:(j + 1) * LRU_BLOCK_W, j * LRU_BLOCK_W:(j + 1) * LRU_BLOCK_W].set(w[b])
    return out


def _head_perm():
    cols = []
    for s in range(KV_GROUP):
        cols += list(range(s * HEAD_DIM, (s + 1) * HEAD_DIM))
        cols += list(range((KV_GROUP + s) * HEAD_DIM, (KV_GROUP + s + 1) * HEAD_DIM))
    return jnp.asarray(cols, dtype=jnp.int32)


def kernel(x, c, w_ada, b_ada, g_mix, w_in, conv_w, conv_b, w_rg_a, b_rg_a, w_rg_x, b_rg_x, lru_lambda, sinks,
           g_attn_out, g_lru_out, w_out, g_ffn, w_router, b_router, w_gu, b_gu, w_down, b_down, g_final):
    B, S, D = x.shape
    T = B * S
    depth = w_ada.shape[0]
    perm = _head_perm()
    x2 = x.reshape(T, D)
    for l in range(depth):
        mod3 = _adaln(c, w_ada[l], b_ada[l]).reshape(B, 1, 6 * D)
        w_in_l = jnp.concatenate([w_in[l][:, :ATTN_WIDTH][:, perm], w_in[l][:, ATTN_WIDTH:]], axis=1).astype(BF16)
        q, k, v, xr, xg = _in_proj(x2, mod3, g_mix[l], w_in_l, S)
        attn = _attention(q, k, v, sinks[l], B, S)
        lru = _rglru(xr, xg, conv_w[l], conv_b[l], _block_diag(w_rg_a[l]).astype(BF16), b_rg_a[l].reshape(-1),
                     _block_diag(w_rg_x[l]).astype(BF16), b_rg_x[l].reshape(-1), lru_lambda[l], B, S)
        w_out_l = jnp.concatenate([w_out[l][:ATTN_WIDTH][perm], w_out[l][ATTN_WIDTH:]], axis=0).astype(BF16)
        x1, h2, topi, gcol = _out_router(attn, lru, x2, mod3, g_attn_out[l][perm], g_lru_out[l], w_out_l,
                                         g_ffn[l], w_router[l].T, b_router[l], S)
        blk_e, row_tok, row_dst = _routing_tables(topi, T)
        yt = _moe(blk_e, row_tok, row_dst, h2, w_gu[l], b_gu[l], w_down[l], b_down[l],
                  T * TOP_K + 2 * MOE_ROWS)
        x2 = _combine(x1, yt, gcol, mod3, g_final, S, apply_final=(l == depth - 1))
    return x2.reshape(B, S, D)
```
